```python
import math, functools
import jax, jax.numpy as jnp
from jax import lax
import numpy as np

D_MODEL = 1024
BATCH = 2
SEQ = 8192
DEPTH = 1
DEC_BATCH = 128
DEC_SEQ = 1
PAST_LEN = 8192
PAGE_SIZE = 128

HEAD_DIM = 64
N_HEADS = 8
N_KV_HEADS = 2
GQA_GROUP = N_HEADS // N_KV_HEADS
ATTN_W = N_HEADS * HEAD_DIM
KV_W = N_KV_HEADS * HEAD_DIM
WINDOW = 128
ATTN_BLOCK = WINDOW
ROPE_THETA = 10000.0
CHUNK = 128
N_GROUPS = 4
GROUP_CH = 128
GMLP_W = N_GROUPS * GROUP_CH
N_EXPERTS = 32
TOP_K = 4
D_FF = D_MODEL
SWIGLU_LIMIT = 7.0
SWIGLU_ALPHA = 1.702
MOE_BLOCK = 128
IN_SPLITS = [ATTN_W, ATTN_W + KV_W, ATTN_W + 2 * KV_W, ATTN_W + 2 * KV_W + GMLP_W,
             ATTN_W + 2 * KV_W + 2 * GMLP_W, ATTN_W + 2 * KV_W + 2 * GMLP_W + D_MODEL]
IN_W = ATTN_W + 2 * KV_W + 2 * GMLP_W + 2 * D_MODEL
DN_ALPHA = (2 * DEPTH) ** 0.25
DN_BETA = (8 * DEPTH) ** -0.25
LN_EPS = 1e-5
NEG = -1e30

kernel_name = 'hybrid_gmlp_swa_sink_moe_step'


def layer_norm(x, g, b):
    xf = x.astype(jnp.float32)
    mu = jnp.mean(xf, -1, keepdims=True)
    var = jnp.mean(jnp.square(xf - mu), -1, keepdims=True)
    y = (xf - mu) * lax.rsqrt(var + LN_EPS)
    return (y * g.astype(jnp.float32) + b.astype(jnp.float32)).astype(x.dtype)


def rope(x, pos):
    half = HEAD_DIM // 2
    inv = ROPE_THETA ** (-jnp.arange(half, dtype=jnp.float32) / half)
    ang = pos.astype(jnp.float32)[:, None] * inv[None, :]
    cos = jnp.cos(ang)[None, :, None, :]
    sin = jnp.sin(ang)[None, :, None, :]
    xf = x.astype(jnp.float32)
    x1, x2 = xf[..., :half], xf[..., half:]
    return jnp.concatenate([x1 * cos - x2 * sin, x2 * cos + x1 * sin], -1).astype(x.dtype)


def sink_softmax(scores, mask, sinks):
    sink = sinks.astype(jnp.float32).reshape(N_KV_HEADS, GQA_GROUP, 1, 1)
    s = jnp.where(mask, scores, NEG)
    m = jnp.maximum(jnp.max(s, -1, keepdims=True), sink)
    p = jnp.exp(s - m)
    return p / (jnp.sum(p, -1, keepdims=True) + jnp.exp(sink - m))


def attn_prompt(q, k, v, sinks):
    B, S = q.shape[:2]
    NB = S // ATTN_BLOCK
    qb = q.reshape(B, NB, ATTN_BLOCK, N_KV_HEADS, GQA_GROUP, HEAD_DIM)
    kb = k.reshape(B, NB, ATTN_BLOCK, N_KV_HEADS, HEAD_DIM)
    vb = v.reshape(B, NB, ATTN_BLOCK, N_KV_HEADS, HEAD_DIM)
    pad = ((0, 0), (1, 0), (0, 0), (0, 0), (0, 0))
    kk = jnp.concatenate([jnp.pad(kb, pad)[:, :-1], kb], 2)
    vv = jnp.concatenate([jnp.pad(vb, pad)[:, :-1], vb], 2)
    scores = jnp.einsum('bnqkgd,bnskd->bnkgqs', qb, kk).astype(jnp.float32) * (HEAD_DIM ** -0.5)
    qi = jnp.arange(ATTN_BLOCK)[:, None]
    kj = jnp.arange(2 * ATTN_BLOCK)[None, :]
    diff = qi + ATTN_BLOCK - kj
    band = (diff >= 0) & (diff < WINDOW)
    valid = (jnp.arange(NB)[:, None, None] > 0) | (kj >= ATTN_BLOCK)[None]
    mask = (band[None] & valid)[None, :, None, None]
    p = sink_softmax(scores, mask, sinks).astype(vv.dtype)
    o = jnp.einsum('bnkgqs,bnskd->bnqkgd', p, vv)
    return o.reshape(B, S, ATTN_W)


def attn_sample(q, k, v, k_buf, v_buf, sinks):
    DB, L = q.shape[:2]
    Wb = k_buf.shape[1]
    kk = jnp.concatenate([k_buf.astype(k.dtype), k], 1)
    vv = jnp.concatenate([v_buf.astype(v.dtype), v], 1)
    qg = q.reshape(DB, L, N_KV_HEADS, GQA_GROUP, HEAD_DIM)
    scores = jnp.einsum('blkgd,bskd->bkgls', qg, kk).astype(jnp.float32) * (HEAD_DIM ** -0.5)
    diff = (jnp.arange(L)[:, None] + Wb) - jnp.arange(Wb + L)[None, :]
    mask = (diff >= 0) & (diff < WINDOW)
    p = sink_softmax(scores, mask, sinks).astype(vv.dtype)
    o = jnp.einsum('bkgls,bskd->blkgd', p, vv)
    return o.reshape(DB, L, ATTN_W)


def spatial_gate(vn, w_s, b_s):
    B, L, _ = vn.shape
    if L < CHUNK:
        Lc, padn = L, 0
    else:
        Lc, padn = CHUNK, (-L) % CHUNK
    vp = jnp.pad(vn, ((0, 0), (0, padn), (0, 0)))
    nc = (L + padn) // Lc
    vr = vp.reshape(B, nc, Lc, N_GROUPS, GROUP_CH)
    tril = jnp.tril(jnp.ones((Lc, Lc), dtype=bool))
    w = jnp.where(tril[None], w_s[:, :Lc, :Lc], 0).astype(vn.dtype)
    s = jnp.einsum('gts,bnsgc->bntgc', w, vr) + jnp.transpose(b_s[:, :Lc])[None, None, :, :, None].astype(vn.dtype)
    return s.reshape(B, nc * Lc, GMLP_W)[:, :L]


def swiglu_expert(xb, w1e, b1e, w2e, b2e):
    z = xb @ w1e + b1e
    zg, zl = z[:, :D_FF], z[:, D_FF:]
    zg = jnp.minimum(zg, SWIGLU_LIMIT)
    zl = jnp.clip(zl, -SWIGLU_LIMIT, SWIGLU_LIMIT)
    act = zg * jax.nn.sigmoid(SWIGLU_ALPHA * zg) * (zl + 1)
    return act @ w2e + b2e


def moe_ffn(h, w_router, b_router, w1, b1, w2, b2):
    T, D = h.shape
    logits = (h @ w_router + b_router).astype(jnp.float32)
    top_val, top_idx = lax.top_k(logits, TOP_K)
    gates = jax.nn.softmax(top_val, axis=-1)
    A = T * TOP_K
    flat_e = top_idx.reshape(A).astype(jnp.int32)
    flat_tok = jnp.arange(A, dtype=jnp.int32) // TOP_K
    order = jnp.argsort(flat_e).astype(jnp.int32)
    sorted_e = flat_e[order]
    counts = jnp.bincount(flat_e, length=N_EXPERTS).astype(jnp.int32)
    start = jnp.cumsum(counts) - counts
    padded = (counts + MOE_BLOCK - 1) // MOE_BLOCK * MOE_BLOCK
    pad_end = jnp.cumsum(padded)
    pad_start = pad_end - padded
    dest_sorted = pad_start[sorted_e] + jnp.arange(A, dtype=jnp.int32) - start[sorted_e]
    n_blocks = -(-(A + N_EXPERTS * (MOE_BLOCK - 1)) // MOE_BLOCK)
    R = n_blocks * MOE_BLOCK
    row_tok = jnp.full((R,), T, jnp.int32).at[dest_sorted].set(flat_tok[order])
    block_e = jnp.minimum(jnp.searchsorted(pad_end, jnp.arange(n_blocks, dtype=jnp.int32) * MOE_BLOCK, side='right'), N_EXPERTS - 1)
    h_pad = jnp.concatenate([h, jnp.zeros((1, D), h.dtype)], 0)

    def run_block(args):
        toks, e = args
        return swiglu_expert(h_pad[toks], w1[e], b1[e], w2[e], b2[e])

    out = lax.map(run_block, (row_tok.reshape(n_blocks, MOE_BLOCK), block_e)).reshape(R, D)
    dest = jnp.zeros((A,), jnp.int32).at[order].set(dest_sorted)
    y = out[dest].reshape(T, TOP_K, D)
    return jnp.einsum('tk,tkd->td', gates.astype(y.dtype), y)


def layer_forward(x, c, pos, attend, w_ada, b_ada, w_in, b_in, ln_v_g, ln_v_b, w_s, b_s,
                  w_pa, w_pb, w_o, b_o, ln1_g, ln1_b, ln2_g, ln2_b,
                  w_router, b_router, w1, b1, w2, b2):
    B, L, D = x.shape
    mod = (jax.nn.silu(c) @ w_ada + b_ada)[:, None, :]
    sh1, sc1, g1, sh2, sc2, g2 = jnp.split(mod, 6, axis=-1)
    h = x * (1 + sc1) + sh1
    proj = h @ w_in + b_in
    q, k, v, gu, gv, ga, gb = jnp.split(proj, IN_SPLITS, axis=-1)
    q = rope(q.reshape(B, L, N_HEADS, HEAD_DIM), pos)
    k = rope(k.reshape(B, L, N_KV_HEADS, HEAD_DIM), pos)
    v = v.reshape(B, L, N_KV_HEADS, HEAD_DIM)
    o_attn = attend(q, k, v)
    u = jax.nn.gelu(gu)
    vn = layer_norm(jax.nn.gelu(gv), ln_v_g, ln_v_b)
    o_gmlp = u * spatial_gate(vn, w_s, b_s)
    merged = jax.nn.sigmoid(ga) * (o_gmlp @ w_pa) + jax.nn.sigmoid(gb) * (o_attn @ w_pb)
    mix = merged @ w_o + b_o
    x = layer_norm(DN_ALPHA * x + g1 * mix, ln1_g, ln1_b)
    h2 = x * (1 + sc2) + sh2
    ff = moe_ffn(h2.reshape(B * L, D), w_router, b_router, w1, b1, w2, b2).reshape(B, L, D)
    x = layer_norm(DN_ALPHA * x + g2 * ff, ln2_g, ln2_b)
    return x, k, v, vn


def setup_inputs(seed: int = 0) -> dict:
    key = jax.random.key(seed)
    ks = jax.random.split(key, 32)
    f32 = jnp.float32

    def nrm(k, shape, scale):
        return jax.random.normal(k, shape, f32) * scale

    wb = min(WINDOW, PAST_LEN)
    return {
        'x_prompt': nrm(ks[0], (BATCH, SEQ, D_MODEL), 1.0),
        'x_sample': nrm(ks[1], (DEC_BATCH, DEC_SEQ, D_MODEL), 1.0),
        'cache_k_win': nrm(ks[2], (DEPTH, DEC_BATCH, wb, N_KV_HEADS, HEAD_DIM), 1.0),
        'cache_v_win': nrm(ks[3], (DEPTH, DEC_BATCH, wb, N_KV_HEADS, HEAD_DIM), 1.0),
        'c_prompt': nrm(ks[4], (BATCH, D_MODEL), 1.0),
        'c_sample': nrm(ks[5], (DEC_BATCH, D_MODEL), 1.0),
        'w_ada': nrm(ks[6], (DEPTH, D_MODEL, 6 * D_MODEL), 0.5 * D_MODEL ** -0.5),
        'b_ada': nrm(ks[7], (DEPTH, 6 * D_MODEL), 0.01),
        'w_in': nrm(ks[8], (DEPTH, D_MODEL, IN_W), D_MODEL ** -0.5),
        'b_in': nrm(ks[9], (DEPTH, IN_W), 0.01),
        'sinks': nrm(ks[10], (DEPTH, N_HEADS), 0.5),
        'ln_v_g': 1.0 + nrm(ks[11], (DEPTH, GMLP_W), 0.02),
        'ln_v_b': nrm(ks[12], (DEPTH, GMLP_W), 0.01),
        'w_s': nrm(ks[13], (DEPTH, N_GROUPS, CHUNK, CHUNK), CHUNK ** -0.5),
        'b_s': 1.0 + nrm(ks[14], (DEPTH, N_GROUPS, CHUNK), 0.1),
        'w_pa': nrm(ks[15], (DEPTH, GMLP_W, D_MODEL), GMLP_W ** -0.5),
        'w_pb': nrm(ks[16], (DEPTH, ATTN_W, D_MODEL), ATTN_W ** -0.5),
        'w_o': nrm(ks[17], (DEPTH, D_MODEL, D_MODEL), DN_BETA * D_MODEL ** -0.5),
        'b_o': nrm(ks[18], (DEPTH, D_MODEL), 0.01),
        'ln1_g': 1.0 + nrm(ks[19], (DEPTH, D_MODEL), 0.02),
        'ln1_b': nrm(ks[20], (DEPTH, D_MODEL), 0.01),
        'ln2_g': 1.0 + nrm(ks[21], (DEPTH, D_MODEL), 0.02),
        'ln2_b': nrm(ks[22], (DEPTH, D_MODEL), 0.01),
        'w_router': nrm(ks[23], (DEPTH, D_MODEL, N_EXPERTS), D_MODEL ** -0.5),
        'b_router': nrm(ks[24], (DEPTH, N_EXPERTS), 0.01),
        'w1': nrm(ks[25], (DEPTH, N_EXPERTS, D_MODEL, 2 * D_FF), D_MODEL ** -0.5),
        'b1': nrm(ks[26], (DEPTH, N_EXPERTS, 2 * D_FF), 0.01),
        'w2': nrm(ks[27], (DEPTH, N_EXPERTS, D_FF, D_MODEL), DN_BETA * D_FF ** -0.5),
        'b2': nrm(ks[28], (DEPTH, N_EXPERTS, D_MODEL), 0.01),
    }


def reference(x_prompt, x_sample, cache_k_win, cache_v_win, c_prompt, c_sample,
              w_ada, b_ada, w_in, b_in, sinks, ln_v_g, ln_v_b, w_s, b_s,
              w_pa, w_pb, w_o, b_o, ln1_g, ln1_b, ln2_g, ln2_b,
              w_router, b_router, w1, b1, w2, b2):
    S = x_prompt.shape[1]
    L = x_sample.shape[1]
    pos_p = jnp.arange(S, dtype=jnp.int32)
    pos_s = PAST_LEN + jnp.arange(L, dtype=jnp.int32)
    wp = min(WINDOW, S)
    xp, xs = x_prompt, x_sample
    kp_rows, vp_rows, ks_rows, vs_rows, gv_rows = [], [], [], [], []
    for l in range(DEPTH):
        lw = (w_ada[l], b_ada[l], w_in[l], b_in[l], ln_v_g[l], ln_v_b[l], w_s[l], b_s[l],
              w_pa[l], w_pb[l], w_o[l], b_o[l], ln1_g[l], ln1_b[l], ln2_g[l], ln2_b[l],
              w_router[l], b_router[l], w1[l], b1[l], w2[l], b2[l])
        att_p = functools.partial(attn_prompt, sinks=sinks[l])
        att_s = functools.partial(attn_sample, k_buf=cache_k_win[l], v_buf=cache_v_win[l], sinks=sinks[l])
        xp, kp, vp, _ = layer_forward(xp, c_prompt, pos_p, att_p, *lw)
        xs, kn, vnew, gvs = layer_forward(xs, c_sample, pos_s, att_s, *lw)
        kp_rows.append(kp[:, S - wp:])
        vp_rows.append(vp[:, S - wp:])
        ks_rows.append(kn)
        vs_rows.append(vnew)
        gv_rows.append(gvs)
    return (xp, xs, jnp.stack(kp_rows), jnp.stack(vp_rows), jnp.stack(ks_rows), jnp.stack(vs_rows), jnp.stack(gv_rows))
```

```python
import functools
import math

import jax
import jax.numpy as jnp
from jax import lax
from jax.experimental import pallas as pl
from jax.experimental.pallas import tpu as pltpu

F32 = jnp.float32
BF16 = jnp.bfloat16

D_MODEL = 1024
HEAD_DIM = 64
N_HEADS = 8
N_KV_HEADS = 2
GQA_GROUP = N_HEADS // N_KV_HEADS
ATTN_W = N_HEADS * HEAD_DIM
KV_W = N_KV_HEADS * HEAD_DIM
WINDOW = 128
ROPE_THETA = 10000.0
CHUNK = 128
N_GROUPS = 4
GROUP_CH = 128
GMLP_W = N_GROUPS * GROUP_CH
N_EXPERTS = 32
TOP_K = 4
D_FF = D_MODEL
SWIGLU_LIMIT = 7.0
SWIGLU_ALPHA = 1.702
PAST_LEN = 8192
LN_EPS = 1e-5
NEG = -1e30
DEPTH = 1
DN_ALPHA = (2 * DEPTH) ** 0.25

OFF_K = ATTN_W
OFF_V = OFF_K + KV_W
OFF_GU = OFF_V + KV_W
OFF_GV = OFF_GU + GMLP_W
OFF_GA = OFF_GV + GMLP_W
OFF_GB = OFF_GA + D_MODEL
IN_W = OFF_GB + D_MODEL

TM = 256
TD = 128
BM = 256
META_W = 8
VMEM_LIMIT = 56 * 1024 * 1024


def _dot(a, b):
    return jnp.dot(a, b, preferred_element_type=F32)


def _dot_nt(a, b):
    return lax.dot_general(a, b, (((1,), (1,)), ((), ())), preferred_element_type=F32)


def _layer_norm(x, g, b):
    mu = jnp.mean(x, axis=-1, keepdims=True)
    xc = x - mu
    var = jnp.mean(xc * xc, axis=-1, keepdims=True)
    return xc * lax.rsqrt(var + LN_EPS) * g + b


def _gelu_tanh(x):
    c = math.sqrt(2.0 / math.pi)
    return 0.5 * x * (1.0 + jnp.tanh(c * (x + 0.044715 * (x * x * x))))


def _rope(x, cos, sin):
    n = x.shape[-1]
    lane = lax.broadcasted_iota(jnp.int32, x.shape, 1)
    first_half = (lane & (HEAD_DIM // 2)) == 0
    swapped = jnp.where(first_half, pltpu.roll(x, n - HEAD_DIM // 2, 1), pltpu.roll(x, HEAD_DIM // 2, 1))
    return x * cos + swapped * sin


def _mixer_tail(x, o_gmlp, o_attn, ga, gb, g1, sc2, sh2, w_pa, w_pb, w_o, b_o, ln1_g, ln1_b):
    a = _dot(o_gmlp.astype(BF16), w_pa[...])
    b = _dot(o_attn.astype(BF16), w_pb[...])
    merged = jax.nn.sigmoid(ga) * a + jax.nn.sigmoid(gb) * b
    mix = _dot(merged.astype(BF16), w_o[...]) + b_o[...]
    x1 = _layer_norm(DN_ALPHA * x + g1 * mix, ln1_g[...], ln1_b[...])
    h2 = x1 * (1.0 + sc2) + sh2
    return x1, h2


def _route(h2, wr_hi, wr_lo, br, base):
    rows = h2.shape[0]
    h_hi = h2.astype(BF16)
    h_lo = (h2 - h_hi.astype(F32)).astype(BF16)
    logits = _dot(h_hi, wr_hi[...]) + _dot(h_lo, wr_hi[...]) + _dot(h_hi, wr_lo[...]) + br[...]
    lane = lax.broadcasted_iota(jnp.int32, (rows, N_EXPERTS), 1).astype(F32)
    vals, idxs, onehots = [], [], []
    l = logits
    for _ in range(TOP_K):
        m = jnp.max(l, axis=-1, keepdims=True)
        idx = jnp.min(jnp.where(l == m, lane, float(N_EXPERTS)), axis=-1, keepdims=True)
        oh = lane == idx
        l = jnp.where(oh, -jnp.inf, l)
        vals.append(m)
        idxs.append(idx)
        onehots.append(oh)
    ps = [jnp.exp(v - vals[0]) for v in vals]
    den = ps[0] + ps[1] + ps[2] + ps[3]
    sel = jnp.zeros((rows, N_EXPERTS), F32)
    for oh in onehots:
        sel = sel + oh.astype(F32)
    r_i = lax.broadcasted_iota(jnp.int32, (rows, rows), 0)
    c_i = lax.broadcasted_iota(jnp.int32, (rows, rows), 1)
    tri = jnp.where(r_i > c_i, 1.0, 0.0).astype(BF16)
    tot = base + _dot(tri, sel.astype(BF16))
    lane8 = lax.broadcasted_iota(jnp.int32, (rows, META_W), 1)
    meta = jnp.zeros((rows, META_W), F32)
    gates = jnp.zeros((rows, META_W), F32)
    for k in range(TOP_K):
        rank = jnp.sum(jnp.where(onehots[k], tot, 0.0), axis=-1, keepdims=True)
        meta = jnp.where(lane8 == k, idxs[k], meta)
        meta = jnp.where(lane8 == TOP_K + k, rank, meta)
        gates = jnp.where(lane8 == k, ps[k] / den, gates)
    new_base = base + jnp.sum(sel, axis=0, keepdims=True)
    return meta.astype(jnp.int32), gates, new_base


def _ada_kernel(c_ref, w_ref, b_ref, o_ref):
    c = c_ref[...]
    s = c * jax.nn.sigmoid(c)
    o_ref[...] = _dot(s.astype(BF16), w_ref[...].astype(BF16)) + b_ref[...]


def _ada(c_all, w_ada, b_ada):
    rows = c_all.shape[0]
    n = w_ada.shape[1]
    tn = 1024
    return pl.pallas_call(
        _ada_kernel,
        grid=(n // tn,),
        in_specs=[pl.BlockSpec((rows, D_MODEL), lambda j: (0, 0)),
                  pl.BlockSpec((D_MODEL, tn), lambda j: (0, j)),
                  pl.BlockSpec((1, tn), lambda j: (0, j))],
        out_specs=pl.BlockSpec((rows, tn), lambda j: (0, j)),
        out_shape=jax.ShapeDtypeStruct((rows, n), F32),
        name="ada",
    )(c_all, w_ada, b_ada)


def _front_prompt_kernel(sinks_ref, x_ref, mod_ref, cos_ref, sin_ref, w_in, b_in, lnv_g, lnv_b, ws_ref,
                         bs_ref, w_pa, w_pb, w_o, b_o, ln1_g, ln1_b, wr_hi, wr_lo, br,
                         x1_ref, h2_ref, meta_ref, gates_ref, kwin_ref, vwin_ref, cnt_ref,
                         prevk, prevv, cnt_scr):
    b = pl.program_id(0)
    i = pl.program_id(1)

    @pl.when(i == 0)
    def _():
        prevk[...] = jnp.zeros_like(prevk)
        prevv[...] = jnp.zeros_like(prevv)

    @pl.when((b == 0) & (i == 0))
    def _():
        cnt_scr[...] = jnp.zeros_like(cnt_scr)

    x = x_ref[...]
    mod = mod_ref[...]
    sh1, sc1, g1 = mod[:, 0:D_MODEL], mod[:, D_MODEL:2 * D_MODEL], mod[:, 2 * D_MODEL:3 * D_MODEL]
    sh2, sc2, g2 = mod[:, 3 * D_MODEL:4 * D_MODEL], mod[:, 4 * D_MODEL:5 * D_MODEL], mod[:, 5 * D_MODEL:]
    del g2
    h = (x * (1.0 + sc1) + sh1).astype(BF16)

    cos = cos_ref[...]
    sin = sin_ref[...]
    qkv = _dot(h, w_in[:, 0:OFF_GU]) + b_in[:, 0:OFF_GU]
    q = _rope(qkv[:, 0:ATTN_W], jnp.tile(cos, (1, ATTN_W // 128)), jnp.tile(sin, (1, ATTN_W // 128)))
    k = _rope(qkv[:, OFF_K:OFF_V], cos, sin)
    v = qkv[:, OFF_V:OFF_GU]
    qs = (q * (HEAD_DIM ** -0.5)).astype(BF16)

    qi = lax.broadcasted_iota(jnp.int32, (WINDOW, 2 * WINDOW), 0)
    kj = lax.broadcasted_iota(jnp.int32, (WINDOW, 2 * WINDOW), 1)
    diff = qi + WINDOW - kj
    band = (diff >= 0) & (diff < WINDOW)
    o_blocks = []
    for j in range(TM // WINDOW):
        r0 = j * WINDOW
        if j == 0:
            k_prev, v_prev = prevk[...], prevv[...]
            prev_ok = i > 0
            mask = band & ((kj >= WINDOW) | prev_ok)
        else:
            k_prev, v_prev = k[r0 - WINDOW:r0], v[r0 - WINDOW:r0]
            mask = band
        kk = jnp.concatenate([k_prev, k[r0:r0 + WINDOW]], axis=0).astype(BF16)
        vv = jnp.concatenate([v_prev, v[r0:r0 + WINDOW]], axis=0).astype(BF16)
        heads = []
        for c in range(N_KV_HEADS):
            kc = kk[:, c * HEAD_DIM:(c + 1) * HEAD_DIM]
            vc = vv[:, c * HEAD_DIM:(c + 1) * HEAD_DIM]
            qg = jnp.concatenate(
                [qs[r0:r0 + WINDOW, (c * GQA_GROUP + g) * HEAD_DIM:(c * GQA_GROUP + g + 1) * HEAD_DIM]
                 for g in range(GQA_GROUP)], axis=0)
            s = _dot_nt(qg, kc)
            ps = []
            for g in range(GQA_GROUP):
                sg = jnp.where(mask, s[g * WINDOW:(g + 1) * WINDOW], NEG)
                sink = sinks_ref[c * GQA_GROUP + g]
                m = jnp.maximum(jnp.max(sg, axis=-1, keepdims=True), sink)
                p = jnp.exp(sg - m)
                den = jnp.sum(p, axis=-1, keepdims=True) + jnp.exp(sink - m)
                ps.append(p / den)
            o = _dot(jnp.concatenate(ps, axis=0).astype(BF16), vc)
            for g in range(GQA_GROUP):
                heads.append(o[g * WINDOW:(g + 1) * WINDOW])
        o_blocks.append(jnp.concatenate(heads, axis=1))
    o_attn = jnp.concatenate(o_blocks, axis=0)

    prevk[...] = k[TM - WINDOW:]
    prevv[...] = v[TM - WINDOW:]
    kwin_ref[...] = k[TM - WINDOW:]
    vwin_ref[...] = v[TM - WINDOW:]

    u = _gelu_tanh(_dot(h, w_in[:, OFF_GU:OFF_GV]) + b_in[:, OFF_GU:OFF_GV])
    vn = _layer_norm(_gelu_tanh(_dot(h, w_in[:, OFF_GV:OFF_GA]) + b_in[:, OFF_GV:OFF_GA]),
                     lnv_g[...], lnv_b[...])
    ti = lax.broadcasted_iota(jnp.int32, (CHUNK, CHUNK), 0)
    si = lax.broadcasted_iota(jnp.int32, (CHUNK, CHUNK), 1)
    vnb = vn.astype(BF16)
    s_chunks = []
    for j in range(TM // CHUNK):
        r0 = j * CHUNK
        cols = []
        for g in range(N_GROUPS):
            wg = jnp.where(ti >= si, ws_ref[g], 0.0).astype(BF16)
            cols.append(_dot(wg, vnb[r0:r0 + CHUNK, g * GROUP_CH:(g + 1) * GROUP_CH]))
        s_chunks.append(jnp.concatenate(cols, axis=1) + bs_ref[...])
    o_gmlp = u * jnp.concatenate(s_chunks, axis=0)

    ga = _dot(h, w_in[:, OFF_GA:OFF_GB]) + b_in[:, OFF_GA:OFF_GB]
    gb = _dot(h, w_in[:, OFF_GB:IN_W]) + b_in[:, OFF_GB:IN_W]
    x1, h2 = _mixer_tail(x, o_gmlp, o_attn, ga, gb, g1, sc2, sh2, w_pa, w_pb, w_o, b_o, ln1_g, ln1_b)
    x1_ref[...] = x1
    h2_ref[...] = h2

    meta, gates, new_base = _route(h2, wr_hi, wr_lo, br, cnt_scr[...])
    meta_ref[...] = meta
    gates_ref[...] = gates
    cnt_scr[...] = new_base
    cnt_ref[...] = new_base


def _const_spec(shape):
    nd = len(shape)
    return pl.BlockSpec(shape, lambda *_: (0,) * nd)


def _front_prompt(x2d, mod_p, cos_t, sin_t, sinks, wts, batch, seq):
    (w_in, b_in, lnv_g, lnv_b, w_s, bs_t, w_pa, w_pb, w_o, b_o, ln1_g, ln1_b, wr_hi, wr_lo, br) = wts
    nt = seq // TM
    t = batch * seq
    row = lambda b, i: (b * nt + i, 0)
    in_specs = [
        pl.BlockSpec(memory_space=pltpu.SMEM),
        pl.BlockSpec((TM, D_MODEL), row),
        pl.BlockSpec((None, 1, 6 * D_MODEL), lambda b, i: (b, 0, 0)),
        pl.BlockSpec((TM, 128), lambda b, i: (i, 0)),
        pl.BlockSpec((TM, 128), lambda b, i: (i, 0)),
    ] + [_const_spec(w.shape) for w in wts]
    out_specs = [
        pl.BlockSpec((TM, D_MODEL), row),
        pl.BlockSpec((TM, D_MODEL), row),
        pl.BlockSpec((TM, META_W), row),
        pl.BlockSpec((TM, META_W), row),
        pl.BlockSpec((None, WINDOW, KV_W), lambda b, i: (b, 0, 0)),
        pl.BlockSpec((None, WINDOW, KV_W), lambda b, i: (b, 0, 0)),
        pl.BlockSpec((1, N_EXPERTS), lambda b, i: (0, 0)),
    ]
    out_shape = [
        jax.ShapeDtypeStruct((t, D_MODEL), F32),
        jax.ShapeDtypeStruct((t, D_MODEL), F32),
        jax.ShapeDtypeStruct((t, META_W), jnp.int32),
        jax.ShapeDtypeStruct((t, META_W), F32),
        jax.ShapeDtypeStruct((batch, WINDOW, KV_W), F32),
        jax.ShapeDtypeStruct((batch, WINDOW, KV_W), F32),
        jax.ShapeDtypeStruct((1, N_EXPERTS), F32),
    ]
    return pl.pallas_call(
        _front_prompt_kernel,
        grid=(batch, nt),
        in_specs=in_specs,
        out_specs=out_specs,
        out_shape=out_shape,
        scratch_shapes=[pltpu.VMEM((WINDOW, KV_W), F32), pltpu.VMEM((WINDOW, KV_W), F32),
                        pltpu.VMEM((1, N_EXPERTS), F32)],
        compiler_params=pltpu.CompilerParams(dimension_semantics=("arbitrary", "arbitrary"),
                                             vmem_limit_bytes=VMEM_LIMIT),
        name="front_prompt",
    )(sinks, x2d, mod_p, cos_t, sin_t, *wts)


def _front_sample_kernel(sinks_ref, x_ref, mod_ref, cos_ref, sin_ref, ck_ref, cv_ref, base_ref, w_in, b_in,
                         lnv_g, lnv_b, ws0_ref, bs0_ref, w_pa, w_pb, w_o, b_o, ln1_g, ln1_b, wr_hi, wr_lo, br,
                         x1_ref, h2_ref, meta_ref, gates_ref, knew_ref, vnew_ref, vn_ref, cnt_ref):
    x = x_ref[...]
    db = x.shape[0]
    mod = mod_ref[...]
    sh1, sc1, g1 = mod[:, 0:D_MODEL], mod[:, D_MODEL:2 * D_MODEL], mod[:, 2 * D_MODEL:3 * D_MODEL]
    sh2, sc2 = mod[:, 3 * D_MODEL:4 * D_MODEL], mod[:, 4 * D_MODEL:5 * D_MODEL]
    h = (x * (1.0 + sc1) + sh1).astype(BF16)

    cos = cos_ref[...]
    sin = sin_ref[...]
    qkv = _dot(h, w_in[:, 0:OFF_GU]) + b_in[:, 0:OFF_GU]
    q = _rope(qkv[:, 0:ATTN_W], jnp.tile(cos, (1, ATTN_W // 128)), jnp.tile(sin, (1, ATTN_W // 128)))
    k = _rope(qkv[:, OFF_K:OFF_V], cos, sin)
    v = qkv[:, OFF_V:OFF_GU]
    knew_ref[...] = k
    vnew_ref[...] = v
    qs = q * (HEAD_DIM ** -0.5)

    is_row0 = lax.broadcasted_iota(jnp.int32, (db, WINDOW, KV_W), 1) == 0
    kk = jnp.where(is_row0, k[:, None, :], ck_ref[...]).astype(BF16)
    vv = jnp.where(is_row0, v[:, None, :], cv_ref[...]).astype(BF16)
    heads = []
    for c in range(N_KV_HEADS):
        kc = kk[:, :, c * HEAD_DIM:(c + 1) * HEAD_DIM]
        vc = vv[:, :, c * HEAD_DIM:(c + 1) * HEAD_DIM]
        qg = jnp.stack([qs[:, (c * GQA_GROUP + g) * HEAD_DIM:(c * GQA_GROUP + g + 1) * HEAD_DIM]
                        for g in range(GQA_GROUP)], axis=1).astype(BF16)
        s = jnp.einsum('bgd,bsd->bgs', qg, kc, preferred_element_type=F32)
        ps = []
        for g in range(GQA_GROUP):
            sg = s[:, g, :]
            sink = sinks_ref[c * GQA_GROUP + g]
            m = jnp.maximum(jnp.max(sg, axis=-1, keepdims=True), sink)
            p = jnp.exp(sg - m)
            den = jnp.sum(p, axis=-1, keepdims=True) + jnp.exp(sink - m)
            ps.append(p / den)
        pm = jnp.stack(ps, axis=1).astype(BF16)
        o = jnp.einsum('bgs,bsd->bgd', pm, vc, preferred_element_type=F32)
        for g in range(GQA_GROUP):
            heads.append(o[:, g, :])
    o_attn = jnp.concatenate(heads, axis=1)

    u = _gelu_tanh(_dot(h, w_in[:, OFF_GU:OFF_GV]) + b_in[:, OFF_GU:OFF_GV])
    vn = _layer_norm(_gelu_tanh(_dot(h, w_in[:, OFF_GV:OFF_GA]) + b_in[:, OFF_GV:OFF_GA]),
                     lnv_g[...], lnv_b[...])
    vn_ref[...] = vn
    o_gmlp = u * (ws0_ref[...] * vn + bs0_ref[...])

    ga = _dot(h, w_in[:, OFF_GA:OFF_GB]) + b_in[:, OFF_GA:OFF_GB]
    gb = _dot(h, w_in[:, OFF_GB:IN_W]) + b_in[:, OFF_GB:IN_W]
    x1, h2 = _mixer_tail(x, o_gmlp, o_attn, ga, gb, g1, sc2, sh2, w_pa, w_pb, w_o, b_o, ln1_g, ln1_b)
    x1_ref[...] = x1
    h2_ref[...] = h2
    meta, gates, new_base = _route(h2, wr_hi, wr_lo, br, base_ref[...])
    meta_ref[...] = meta
    gates_ref[...] = gates
    cnt_ref[...] = new_base


def _front_sample(x2d, mod_s, cos_s, sin_s, ck, cv, base, sinks, wts):
    db = x2d.shape[0]
    args = (x2d, mod_s, cos_s, sin_s, ck, cv, base) + tuple(wts)
    in_specs = [pl.BlockSpec(memory_space=pltpu.SMEM)] + [_const_spec(a.shape) for a in args]
    out_shape = [
        jax.ShapeDtypeStruct((db, D_MODEL), F32),
        jax.ShapeDtypeStruct((db, D_MODEL), F32),
        jax.ShapeDtypeStruct((db, META_W), jnp.int32),
        jax.ShapeDtypeStruct((db, META_W), F32),
        jax.ShapeDtypeStruct((db, KV_W), F32),
        jax.ShapeDtypeStruct((db, KV_W), F32),
        jax.ShapeDtypeStruct((db, GMLP_W), F32),
        jax.ShapeDtypeStruct((1, N_EXPERTS), F32),
    ]
    return pl.pallas_call(
        _front_sample_kernel,
        grid=(1,),
        in_specs=in_specs,
        out_specs=[_const_spec(s.shape) for s in out_shape],
        out_shape=out_shape,
        compiler_params=pltpu.CompilerParams(dimension_semantics=("arbitrary",),
                                             vmem_limit_bytes=VMEM_LIMIT),
        name="front_sample",
    )(sinks, *args)


def _row_copy(src, t, dst, d, sem):
    return pltpu.make_async_copy(src.at[pl.ds(t, 1)], dst.at[pl.ds(d, 1)], sem)


def _dispatch_kernel(starts_ref, meta_ref, hp_ref, hs_ref, xs_ref, sem, *, n_prompt_tiles):
    i = pl.program_id(0)

    def scatter(src):
        def start(t, carry):
            for k in range(TOP_K):
                e = meta_ref[0, t * META_W + k]
                r = meta_ref[0, t * META_W + TOP_K + k]
                _row_copy(src, t, xs_ref, starts_ref[e] + r, sem).start()
            return carry
        lax.fori_loop(0, TD, start, 0)

        def wait(t, carry):
            for k in range(TOP_K):
                _row_copy(src, 0, xs_ref, 0, sem).wait()
            return carry
        lax.fori_loop(0, TD, wait, 0)

    @pl.when(i < n_prompt_tiles)
    def _():
        scatter(hp_ref)

    @pl.when(i >= n_prompt_tiles)
    def _():
        scatter(hs_ref)


def _dispatch(starts, meta_tiles, h2_p, h2_s):
    npt = h2_p.shape[0] // TD
    nst = h2_s.shape[0] // TD
    rows = (h2_p.shape[0] + h2_s.shape[0]) * TOP_K
    grid_spec = pltpu.PrefetchScalarGridSpec(
        num_scalar_prefetch=1,
        grid=(npt + nst,),
        in_specs=[
            pl.BlockSpec((None, 1, TD * META_W), lambda i, s: (i, 0, 0), memory_space=pltpu.SMEM),
            pl.BlockSpec((TD, D_MODEL), lambda i, s: (jnp.minimum(i, npt - 1), 0)),
            pl.BlockSpec((TD, D_MODEL), lambda i, s: (jnp.maximum(i - npt, 0), 0)),
        ],
        out_specs=pl.BlockSpec(memory_space=pl.ANY),
        scratch_shapes=[pltpu.SemaphoreType.DMA(())],
    )
    return pl.pallas_call(
        functools.partial(_dispatch_kernel, n_prompt_tiles=npt),
        grid_spec=grid_spec,
        out_shape=jax.ShapeDtypeStruct((rows, D_MODEL), F32),
        compiler_params=pltpu.CompilerParams(dimension_semantics=("arbitrary",)),
        name="dispatch",
    )(starts, meta_tiles, h2_p, h2_s)


def _moe_kernel(blk_ref, exp_ref, lo_ref, hi_ref, flag_ref, x_ref, w1_ref, b1_ref, w2_ref, b2_ref, o_ref,
                w1b, w2b):
    del blk_ref, exp_ref
    s = pl.program_id(0)
    flag = flag_ref[s]

    @pl.when((flag & 1) != 0)
    def _():
        @pl.when((flag & 4) != 0)
        def _():
            w1b[...] = w1_ref[...].astype(BF16)
            w2b[...] = w2_ref[...].astype(BF16)

        @pl.when((flag & 2) != 0)
        def _():
            o_ref[...] = jnp.zeros_like(o_ref)

        x = x_ref[...].astype(BF16)
        z = _dot(x, w1b[...]) + b1_ref[...]
        zg = jnp.minimum(z[:, :D_FF], SWIGLU_LIMIT)
        zl = jnp.clip(z[:, D_FF:], -SWIGLU_LIMIT, SWIGLU_LIMIT)
        act = zg * jax.nn.sigmoid(SWIGLU_ALPHA * zg) * (zl + 1.0)
        y = _dot(act.astype(BF16), w2b[...]) + b2_ref[...]
        row = lax.broadcasted_iota(jnp.int32, (BM, 1), 0)
        mine = (row >= lo_ref[s]) & (row < hi_ref[s])
        o_ref[...] = jnp.where(mine, y, o_ref[...])


def _moe(tables, xs, w1, b1, w2, b2):
    blk, exp, lo, hi, flag = tables
    nsteps = blk.shape[0]
    grid_spec = pltpu.PrefetchScalarGridSpec(
        num_scalar_prefetch=5,
        grid=(nsteps,),
        in_specs=[
            pl.BlockSpec((BM, D_MODEL), lambda s, blk, exp, lo, hi, fl: (blk[s], 0)),
            pl.BlockSpec((None, D_MODEL, 2 * D_FF), lambda s, blk, exp, lo, hi, fl: (exp[s], 0, 0)),
            pl.BlockSpec((None, 1, 2 * D_FF), lambda s, blk, exp, lo, hi, fl: (exp[s], 0, 0)),
            pl.BlockSpec((None, D_FF, D_MODEL), lambda s, blk, exp, lo, hi, fl: (exp[s], 0, 0)),
            pl.BlockSpec((None, 1, D_MODEL), lambda s, blk, exp, lo, hi, fl: (exp[s], 0, 0)),
        ],
        out_specs=pl.BlockSpec((BM, D_MODEL), lambda s, blk, exp, lo, hi, fl: (blk[s], 0)),
        scratch_shapes=[pltpu.VMEM((D_MODEL, 2 * D_FF), BF16), pltpu.VMEM((D_FF, D_MODEL), BF16)],
    )
    return pl.pallas_call(
        _moe_kernel,
        grid_spec=grid_spec,
        out_shape=jax.ShapeDtypeStruct(xs.shape, F32),
        compiler_params=pltpu.CompilerParams(dimension_semantics=("arbitrary",),
                                             vmem_limit_bytes=VMEM_LIMIT),
        name="moe",
    )(blk, exp, lo, hi, flag, xs, w1, b1, w2, b2)


def _moe_tables(counts, nsteps):
    counts = counts.astype(jnp.int32)
    ends = jnp.cumsum(counts)
    starts = ends - counts
    first_blk = starts // BM
    last_blk = (ends - 1) // BM
    n_e = jnp.where(counts > 0, last_blk - first_blk + 1, 0)
    step_end = jnp.cumsum(n_e)
    step_start = step_end - n_e
    total = step_end[-1]
    s = jnp.arange(nsteps, dtype=jnp.int32)
    valid = s < total
    sc = jnp.minimum(s, total - 1)
    e = jnp.minimum(jnp.searchsorted(step_end, sc, side='right'), N_EXPERTS - 1).astype(jnp.int32)
    blk = first_blk[e] + (sc - step_start[e])
    lo = jnp.maximum(starts[e], blk * BM) - blk * BM
    hi = jnp.minimum(ends[e], (blk + 1) * BM) - blk * BM
    lo = jnp.where(valid, lo, 0)
    hi = jnp.where(valid, hi, 0)
    prev_blk = jnp.concatenate([jnp.full((1,), -1, jnp.int32), blk[:-1]])
    prev_e = jnp.concatenate([jnp.full((1,), -1, jnp.int32), e[:-1]])
    flag = (valid.astype(jnp.int32) + 2 * (valid & (blk != prev_blk)).astype(jnp.int32)
            + 4 * (valid & (e != prev_e)).astype(jnp.int32))
    return starts, (blk.astype(jnp.int32), e, lo.astype(jnp.int32), hi.astype(jnp.int32), flag)


def _combine_kernel(starts_ref, meta_ref, x1_ref, gates_ref, g2_ref, ln2_g, ln2_b, ys_ref, o_ref, rows, sem):
    def start(t, carry):
        for k in range(TOP_K):
            e = meta_ref[0, t * META_W + k]
            r = meta_ref[0, t * META_W + TOP_K + k]
            pltpu.make_async_copy(ys_ref.at[pl.ds(starts_ref[e] + r, 1)], rows.at[k, pl.ds(t, 1)], sem).start()
        return carry
    lax.fori_loop(0, TD, start, 0)

    def wait(t, carry):
        for k in range(TOP_K):
            pltpu.make_async_copy(ys_ref.at[pl.ds(0, 1)], rows.at[k, pl.ds(0, 1)], sem).wait()
        return carry
    lax.fori_loop(0, TD, wait, 0)

    gates = gates_ref[...]
    ff = gates[:, 0:1] * rows[0]
    for k in range(1, TOP_K):
        ff = ff + gates[:, k:k + 1] * rows[k]
    o_ref[...] = _layer_norm(DN_ALPHA * x1_ref[...] + g2_ref[...] * ff, ln2_g[...], ln2_b[...])


def _combine(starts, meta_tiles, x1, gates, g2, g2_spec, ln2_g, ln2_b, ys):
    nt = x1.shape[0] // TD
    grid_spec = pltpu.PrefetchScalarGridSpec(
        num_scalar_prefetch=1,
        grid=(nt,),
        in_specs=[
            pl.BlockSpec((None, 1, TD * META_W), lambda i, s: (i, 0, 0), memory_space=pltpu.SMEM),
            pl.BlockSpec((TD, D_MODEL), lambda i, s: (i, 0)),
            pl.BlockSpec((TD, META_W), lambda i, s: (i, 0)),
            g2_spec,
            pl.BlockSpec((1, D_MODEL), lambda i, s: (0, 0)),
            pl.BlockSpec((1, D_MODEL), lambda i, s: (0, 0)),
            pl.BlockSpec(memory_space=pl.ANY),
        ],
        out_specs=pl.BlockSpec((TD, D_MODEL), lambda i, s: (i, 0)),
        scratch_shapes=[pltpu.VMEM((TOP_K, TD, D_MODEL), F32), pltpu.SemaphoreType.DMA(())],
    )
    return pl.pallas_call(
        _combine_kernel,
        grid_spec=grid_spec,
        out_shape=jax.ShapeDtypeStruct(x1.shape, F32),
        compiler_params=pltpu.CompilerParams(dimension_semantics=("arbitrary",)),
        name="combine",
    )(starts, meta_tiles, x1, gates, g2, ln2_g, ln2_b, ys)


def _rope_tables(pos):
    half = HEAD_DIM // 2
    inv = ROPE_THETA ** (-jnp.arange(half, dtype=F32) / half)
    ang = pos.astype(F32)[:, None] * inv[None, :]
    cos, sin = jnp.cos(ang), jnp.sin(ang)
    cos_t = jnp.tile(jnp.concatenate([cos, cos], -1), (1, 128 // HEAD_DIM))
    sin_t = jnp.tile(jnp.concatenate([-sin, sin], -1), (1, 128 // HEAD_DIM))
    return cos_t, sin_t


def kernel(x_prompt, x_sample, cache_k_win, cache_v_win, c_prompt, c_sample, w_ada, b_ada, w_in, b_in, sinks,
           ln_v_g, ln_v_b, w_s, b_s, w_pa, w_pb, w_o, b_o, ln1_g, ln1_b, ln2_g, ln2_b,
           w_router, b_router, w1, b1, w2, b2):
    assert w_ada.shape[0] == DEPTH
    batch, seq, _ = x_prompt.shape
    db, dl, _ = x_sample.shape
    assert dl == 1 and seq % TM == 0 and db % TD == 0 and cache_k_win.shape[2] == WINDOW
    t_p = batch * seq
    t_all = t_p + db

    n_seq = batch + db
    n_seq_pad = -(-n_seq // 8) * 8
    c_all = jnp.concatenate([c_prompt, c_sample, jnp.zeros((n_seq_pad - n_seq, D_MODEL), F32)], 0)
    mod = _ada(c_all, w_ada[0], b_ada)
    mod_p = mod[:batch].reshape(batch, 1, 6 * D_MODEL)
    mod_s = mod[batch:n_seq]

    wr = w_router[0]
    wr_hi = wr.astype(BF16)
    wr_lo = (wr - wr_hi.astype(F32)).astype(BF16)
    shared = (w_pa[0].astype(BF16), w_pb[0].astype(BF16), w_o[0].astype(BF16), b_o, ln1_g, ln1_b,
              wr_hi, wr_lo, b_router)
    w_in_b = w_in[0].astype(BF16)
    sinks1 = sinks[0]

    cos_p, sin_p = _rope_tables(jnp.arange(seq, dtype=jnp.int32))
    bs_t = jnp.repeat(jnp.transpose(b_s[0]), GROUP_CH, axis=1)
    wts_p = (w_in_b, b_in, ln_v_g, ln_v_b, w_s[0], bs_t) + shared
    x1_p, h2_p, meta_p, gates_p, kwin, vwin, cnt_p = _front_prompt(
        x_prompt.reshape(t_p, D_MODEL), mod_p, cos_p, sin_p, sinks1, wts_p, batch, seq)

    cos_s, sin_s = _rope_tables(jnp.full((1,), PAST_LEN, jnp.int32))
    ws0 = jnp.repeat(w_s[0, :, 0, 0], GROUP_CH)[None, :]
    bs0 = jnp.repeat(b_s[0, :, 0], GROUP_CH)[None, :]
    wts_s = (w_in_b, b_in, ln_v_g, ln_v_b, ws0, bs0) + shared
    x1_s, h2_s, meta_s, gates_s, k_new, v_new, vn_s, cnt_all = _front_sample(
        x_sample.reshape(db, D_MODEL), mod_s, cos_s, sin_s,
        cache_k_win[0].reshape(db, WINDOW, KV_W), cache_v_win[0].reshape(db, WINDOW, KV_W),
        cnt_p, sinks1, wts_s)

    nsteps = t_all * TOP_K // BM + N_EXPERTS - 1
    starts, tables = _moe_tables(cnt_all[0], nsteps)
    meta_tiles = jnp.concatenate([meta_p, meta_s], 0).reshape(t_all // TD, 1, TD * META_W)
    xs = _dispatch(starts, meta_tiles, h2_p, h2_s)
    ys = _moe(tables, xs, w1[0], b1[0][:, None, :], w2[0], b2[0][:, None, :])

    g2_p = mod_p[:, :, 5 * D_MODEL:]
    tiles_per_seq = seq // TD
    y_p = _combine(starts, meta_tiles[:t_p // TD], x1_p, gates_p, g2_p,
                   pl.BlockSpec((None, 1, D_MODEL), lambda i, s: (i // tiles_per_seq, 0, 0)),
                   ln2_g, ln2_b, ys)
    y_s = _combine(starts, meta_tiles[t_p // TD:], x1_s, gates_s, mod_s[:, 5 * D_MODEL:],
                   pl.BlockSpec((TD, D_MODEL), lambda i, s: (i, 0)), ln2_g, ln2_b, ys)

    return (y_p.reshape(batch, seq, D_MODEL),
            y_s.reshape(db, 1, D_MODEL),
            kwin.reshape(1, batch, WINDOW, N_KV_HEADS, HEAD_DIM),
            vwin.reshape(1, batch, WINDOW, N_KV_HEADS, HEAD_DIM),
            k_new.reshape(1, db, 1, N_KV_HEADS, HEAD_DIM),
            v_new.reshape(1, db, 1, N_KV_HEADS, HEAD_DIM),
            vn_s.reshape(1, db, 1, GMLP_W))
```

```python
import functools
import math

import jax
import jax.numpy as jnp
import numpy as np
from jax import lax
from jax.experimental import pallas as pl
from jax.experimental.pallas import tpu as pltpu

F32 = jnp.float32
BF16 = jnp.bfloat16

D_MODEL = 1024
HEAD_DIM = 64
N_HEADS = 8
N_KV_HEADS = 2
GQA_GROUP = N_HEADS // N_KV_HEADS
ATTN_W = N_HEADS * HEAD_DIM
KV_W = N_KV_HEADS * HEAD_DIM
WINDOW = 128
ROPE_THETA = 10000.0
CHUNK = 128
N_GROUPS = 4
GROUP_CH = 128
GMLP_W = N_GROUPS * GROUP_CH
N_EXPERTS = 32
TOP_K = 4
D_FF = D_MODEL
SWIGLU_LIMIT = 7.0
SWIGLU_ALPHA = 1.702
PAST_LEN = 8192
LN_EPS = 1e-5
NEG = -1e30
DEPTH = 1
DN_ALPHA = (2 * DEPTH) ** 0.25

OFF_K = ATTN_W
OFF_V = OFF_K + KV_W
OFF_GU = OFF_V + KV_W
OFF_GV = OFF_GU + GMLP_W
OFF_GA = OFF_GV + GMLP_W
OFF_GB = OFF_GA + D_MODEL
IN_W = OFF_GB + D_MODEL

TM = 256
TD = 128
BM = 256
META_W = 8
DMA_UNROLL = 8
VMEM_LIMIT = 56 * 1024 * 1024


def _dot(a, b):
    return jnp.dot(a, b, preferred_element_type=F32)


def _dot_nt(a, b):
    return lax.dot_general(a, b, (((1,), (1,)), ((), ())), preferred_element_type=F32)


def _layer_norm(x, g, b):
    mu = jnp.mean(x, axis=-1, keepdims=True)
    xc = x - mu
    var = jnp.mean(xc * xc, axis=-1, keepdims=True)
    return xc * lax.rsqrt(var + LN_EPS) * g + b


def _gelu_tanh(x):
    c = math.sqrt(2.0 / math.pi)
    return 0.5 * x * (1.0 + jnp.tanh(c * (x + 0.044715 * (x * x * x))))


def _rope(x, cos, sin):
    n = x.shape[-1]
    lane = lax.broadcasted_iota(jnp.int32, x.shape, 1)
    first_half = (lane & (HEAD_DIM // 2)) == 0
    swapped = jnp.where(first_half, pltpu.roll(x, n - HEAD_DIM // 2, 1), pltpu.roll(x, HEAD_DIM // 2, 1))
    return x * cos + swapped * sin


def _mixer_tail(x, o_gmlp, o_attn, ga, gb, g1, sc2, sh2, w_pa, w_pb, w_o, b_o, ln1_g, ln1_b):
    a = _dot(o_gmlp.astype(BF16), w_pa[...])
    b = _dot(o_attn.astype(BF16), w_pb[...])
    merged = jax.nn.sigmoid(ga) * a + jax.nn.sigmoid(gb) * b
    mix = _dot(merged.astype(BF16), w_o[...]) + b_o[...]
    x1 = _layer_norm(DN_ALPHA * x + g1 * mix, ln1_g[...], ln1_b[...])
    h2 = x1 * (1.0 + sc2) + sh2
    return x1, h2


def _route(h2, wr_hi, wr_lo, br, base):
    rows = h2.shape[0]
    h_hi = h2.astype(BF16)
    h_lo = (h2 - h_hi.astype(F32)).astype(BF16)
    logits = _dot(h_hi, wr_hi[...]) + _dot(h_lo, wr_hi[...]) + _dot(h_hi, wr_lo[...]) + br[...]
    lane = lax.broadcasted_iota(jnp.int32, (rows, N_EXPERTS), 1).astype(F32)
    vals, idxs, onehots = [], [], []
    l = logits
    for _ in range(TOP_K):
        m = jnp.max(l, axis=-1, keepdims=True)
        idx = jnp.min(jnp.where(l == m, lane, float(N_EXPERTS)), axis=-1, keepdims=True)
        oh = lane == idx
        l = jnp.where(oh, -jnp.inf, l)
        vals.append(m)
        idxs.append(idx)
        onehots.append(oh)
    ps = [jnp.exp(v - vals[0]) for v in vals]
    den = ps[0] + ps[1] + ps[2] + ps[3]
    sel = jnp.zeros((rows, N_EXPERTS), F32)
    for oh in onehots:
        sel = sel + oh.astype(F32)
    r_i = lax.broadcasted_iota(jnp.int32, (rows, rows), 0)
    c_i = lax.broadcasted_iota(jnp.int32, (rows, rows), 1)
    tri = jnp.where(r_i > c_i, 1.0, 0.0).astype(BF16)
    tot = base + _dot(tri, sel.astype(BF16))
    lane8 = lax.broadcasted_iota(jnp.int32, (rows, META_W), 1)
    meta = jnp.zeros((rows, META_W), F32)
    gates = jnp.zeros((rows, META_W), F32)
    for k in range(TOP_K):
        rank = jnp.sum(jnp.where(onehots[k], tot, 0.0), axis=-1, keepdims=True)
        meta = jnp.where(lane8 == k, idxs[k], meta)
        meta = jnp.where(lane8 == TOP_K + k, rank, meta)
        gates = jnp.where(lane8 == k, ps[k] / den, gates)
    new_base = base + jnp.sum(sel, axis=0, keepdims=True)
    return meta.astype(jnp.int32), gates, new_base


def _ada_kernel(c_ref, w_ref, b_ref, o_ref):
    c = c_ref[...]
    s = c * jax.nn.sigmoid(c)
    o_ref[...] = _dot(s.astype(BF16), w_ref[...].astype(BF16)) + b_ref[...]


def _ada(c_all, w_ada, b_ada):
    rows = c_all.shape[0]
    n = w_ada.shape[1]
    tn = 1024
    return pl.pallas_call(
        _ada_kernel,
        grid=(n // tn,),
        in_specs=[pl.BlockSpec((rows, D_MODEL), lambda j: (0, 0)),
                  pl.BlockSpec((D_MODEL, tn), lambda j: (0, j)),
                  pl.BlockSpec((1, tn), lambda j: (0, j))],
        out_specs=pl.BlockSpec((rows, tn), lambda j: (0, j)),
        out_shape=jax.ShapeDtypeStruct((rows, n), F32),
        name="ada",
    )(c_all, w_ada, b_ada)


def _front_prompt_kernel(sinks_ref, x_ref, mod_ref, cos_ref, sin_ref, w_in, b_in, lnv_g, lnv_b, ws_ref,
                         bs_ref, w_pa, w_pb, w_o, b_o, ln1_g, ln1_b, wr_hi, wr_lo, br,
                         x1_ref, h2_ref, meta_ref, gates_ref, kwin_ref, vwin_ref, cnt_ref,
                         prevk, prevv, cnt_scr):
    b = pl.program_id(0)
    i = pl.program_id(1)

    @pl.when(i == 0)
    def _():
        prevk[...] = jnp.zeros_like(prevk)
        prevv[...] = jnp.zeros_like(prevv)

    @pl.when((b == 0) & (i == 0))
    def _():
        cnt_scr[...] = jnp.zeros_like(cnt_scr)

    x = x_ref[...]
    mod = mod_ref[...]
    sh1, sc1, g1 = mod[:, 0:D_MODEL], mod[:, D_MODEL:2 * D_MODEL], mod[:, 2 * D_MODEL:3 * D_MODEL]
    sh2, sc2, g2 = mod[:, 3 * D_MODEL:4 * D_MODEL], mod[:, 4 * D_MODEL:5 * D_MODEL], mod[:, 5 * D_MODEL:]
    del g2
    h = (x * (1.0 + sc1) + sh1).astype(BF16)

    cos = cos_ref[...]
    sin = sin_ref[...]
    qkv = _dot(h, w_in[:, 0:OFF_GU]) + b_in[:, 0:OFF_GU]
    q = _rope(qkv[:, 0:ATTN_W], jnp.tile(cos, (1, ATTN_W // 128)), jnp.tile(sin, (1, ATTN_W // 128)))
    k = _rope(qkv[:, OFF_K:OFF_V], cos, sin)
    v = qkv[:, OFF_V:OFF_GU]
    qs = (q * (HEAD_DIM ** -0.5)).astype(BF16)

    qi = lax.broadcasted_iota(jnp.int32, (WINDOW, 2 * WINDOW), 0)
    kj = lax.broadcasted_iota(jnp.int32, (WINDOW, 2 * WINDOW), 1)
    diff = qi + WINDOW - kj
    band = (diff >= 0) & (diff < WINDOW)
    o_blocks = []
    for j in range(TM // WINDOW):
        r0 = j * WINDOW
        if j == 0:
            k_prev, v_prev = prevk[...], prevv[...]
            prev_ok = i > 0
            mask = band & ((kj >= WINDOW) | prev_ok)
        else:
            k_prev, v_prev = k[r0 - WINDOW:r0], v[r0 - WINDOW:r0]
            mask = band
        kk = jnp.concatenate([k_prev, k[r0:r0 + WINDOW]], axis=0).astype(BF16)
        vv = jnp.concatenate([v_prev, v[r0:r0 + WINDOW]], axis=0).astype(BF16)
        heads = []
        for c in range(N_KV_HEADS):
            kc = kk[:, c * HEAD_DIM:(c + 1) * HEAD_DIM]
            vc = vv[:, c * HEAD_DIM:(c + 1) * HEAD_DIM]
            qg = jnp.concatenate(
                [qs[r0:r0 + WINDOW, (c * GQA_GROUP + g) * HEAD_DIM:(c * GQA_GROUP + g + 1) * HEAD_DIM]
                 for g in range(GQA_GROUP)], axis=0)
            s = _dot_nt(qg, kc)
            ps = []
            for g in range(GQA_GROUP):
                sg = jnp.where(mask, s[g * WINDOW:(g + 1) * WINDOW], NEG)
                sink = sinks_ref[c * GQA_GROUP + g]
                m = jnp.maximum(jnp.max(sg, axis=-1, keepdims=True), sink)
                p = jnp.exp(sg - m)
                den = jnp.sum(p, axis=-1, keepdims=True) + jnp.exp(sink - m)
                ps.append(p / den)
            o = _dot(jnp.concatenate(ps, axis=0).astype(BF16), vc)
            for g in range(GQA_GROUP):
                heads.append(o[g * WINDOW:(g + 1) * WINDOW])
        o_blocks.append(jnp.concatenate(heads, axis=1))
    o_attn = jnp.concatenate(o_blocks, axis=0)

    prevk[...] = k[TM - WINDOW:]
    prevv[...] = v[TM - WINDOW:]
    kwin_ref[...] = k[TM - WINDOW:]
    vwin_ref[...] = v[TM - WINDOW:]

    u = _gelu_tanh(_dot(h, w_in[:, OFF_GU:OFF_GV]) + b_in[:, OFF_GU:OFF_GV])
    vn = _layer_norm(_gelu_tanh(_dot(h, w_in[:, OFF_GV:OFF_GA]) + b_in[:, OFF_GV:OFF_GA]),
                     lnv_g[...], lnv_b[...])
    ti = lax.broadcasted_iota(jnp.int32, (CHUNK, CHUNK), 0)
    si = lax.broadcasted_iota(jnp.int32, (CHUNK, CHUNK), 1)
    vnb = vn.astype(BF16)
    s_chunks = []
    for j in range(TM // CHUNK):
        r0 = j * CHUNK
        cols = []
        for g in range(N_GROUPS):
            wg = jnp.where(ti >= si, ws_ref[g], 0.0).astype(BF16)
            cols.append(_dot(wg, vnb[r0:r0 + CHUNK, g * GROUP_CH:(g + 1) * GROUP_CH]))
        s_chunks.append(jnp.concatenate(cols, axis=1) + bs_ref[...])
    o_gmlp = u * jnp.concatenate(s_chunks, axis=0)

    ga = _dot(h, w_in[:, OFF_GA:OFF_GB]) + b_in[:, OFF_GA:OFF_GB]
    gb = _dot(h, w_in[:, OFF_GB:IN_W]) + b_in[:, OFF_GB:IN_W]
    x1, h2 = _mixer_tail(x, o_gmlp, o_attn, ga, gb, g1, sc2, sh2, w_pa, w_pb, w_o, b_o, ln1_g, ln1_b)
    x1_ref[...] = x1
    h2_ref[...] = h2

    meta, gates, new_base = _route(h2, wr_hi, wr_lo, br, cnt_scr[...])
    meta_ref[...] = meta
    gates_ref[...] = gates
    cnt_scr[...] = new_base
    cnt_ref[...] = new_base


def _const_spec(shape):
    nd = len(shape)
    return pl.BlockSpec(shape, lambda *_: (0,) * nd)


def _front_prompt(x2d, mod_p, cos_t, sin_t, sinks, wts, batch, seq):
    (w_in, b_in, lnv_g, lnv_b, w_s, bs_t, w_pa, w_pb, w_o, b_o, ln1_g, ln1_b, wr_hi, wr_lo, br) = wts
    nt = seq // TM
    t = batch * seq
    row = lambda b, i: (b * nt + i, 0)
    in_specs = [
        pl.BlockSpec(memory_space=pltpu.SMEM),
        pl.BlockSpec((TM, D_MODEL), row),
        pl.BlockSpec((None, 1, 6 * D_MODEL), lambda b, i: (b, 0, 0)),
        pl.BlockSpec((TM, 128), lambda b, i: (i, 0)),
        pl.BlockSpec((TM, 128), lambda b, i: (i, 0)),
    ] + [_const_spec(w.shape) for w in wts]
    out_specs = [
        pl.BlockSpec((TM, D_MODEL), row),
        pl.BlockSpec((TM, D_MODEL), row),
        pl.BlockSpec((TM, META_W), row),
        pl.BlockSpec((TM, META_W), row),
        pl.BlockSpec((None, WINDOW, KV_W), lambda b, i: (b, 0, 0)),
        pl.BlockSpec((None, WINDOW, KV_W), lambda b, i: (b, 0, 0)),
        pl.BlockSpec((1, N_EXPERTS), lambda b, i: (0, 0)),
    ]
    out_shape = [
        jax.ShapeDtypeStruct((t, D_MODEL), F32),
        jax.ShapeDtypeStruct((t, D_MODEL), F32),
        jax.ShapeDtypeStruct((t, META_W), jnp.int32),
        jax.ShapeDtypeStruct((t, META_W), F32),
        jax.ShapeDtypeStruct((batch, WINDOW, KV_W), F32),
        jax.ShapeDtypeStruct((batch, WINDOW, KV_W), F32),
        jax.ShapeDtypeStruct((1, N_EXPERTS), F32),
    ]
    return pl.pallas_call(
        _front_prompt_kernel,
        grid=(batch, nt),
        in_specs=in_specs,
        out_specs=out_specs,
        out_shape=out_shape,
        scratch_shapes=[pltpu.VMEM((WINDOW, KV_W), F32), pltpu.VMEM((WINDOW, KV_W), F32),
                        pltpu.VMEM((1, N_EXPERTS), F32)],
        compiler_params=pltpu.CompilerParams(dimension_semantics=("arbitrary", "arbitrary"),
                                             vmem_limit_bytes=VMEM_LIMIT),
        name="front_prompt",
    )(sinks, x2d, mod_p, cos_t, sin_t, *wts)


def _front_sample_kernel(sinks_ref, x_ref, mod_ref, cos_ref, sin_ref, ck_ref, cv_ref, base_ref, w_in, b_in,
                         lnv_g, lnv_b, ws0_ref, bs0_ref, w_pa, w_pb, w_o, b_o, ln1_g, ln1_b, wr_hi, wr_lo, br,
                         x1_ref, h2_ref, meta_ref, gates_ref, knew_ref, vnew_ref, vn_ref, cnt_ref):
    x = x_ref[...]
    db = x.shape[0]
    mod = mod_ref[...]
    sh1, sc1, g1 = mod[:, 0:D_MODEL], mod[:, D_MODEL:2 * D_MODEL], mod[:, 2 * D_MODEL:3 * D_MODEL]
    sh2, sc2 = mod[:, 3 * D_MODEL:4 * D_MODEL], mod[:, 4 * D_MODEL:5 * D_MODEL]
    h = (x * (1.0 + sc1) + sh1).astype(BF16)

    cos = cos_ref[...]
    sin = sin_ref[...]
    qkv = _dot(h, w_in[:, 0:OFF_GU]) + b_in[:, 0:OFF_GU]
    q = _rope(qkv[:, 0:ATTN_W], jnp.tile(cos, (1, ATTN_W // 128)), jnp.tile(sin, (1, ATTN_W // 128)))
    k = _rope(qkv[:, OFF_K:OFF_V], cos, sin)
    v = qkv[:, OFF_V:OFF_GU]
    knew_ref[...] = k
    vnew_ref[...] = v
    qs = q * (HEAD_DIM ** -0.5)

    is_row0 = lax.broadcasted_iota(jnp.int32, (db, WINDOW, KV_W), 1) == 0
    kk = jnp.where(is_row0, k[:, None, :], ck_ref[...]).astype(BF16)
    vv = jnp.where(is_row0, v[:, None, :], cv_ref[...]).astype(BF16)
    heads = []
    for c in range(N_KV_HEADS):
        kc = kk[:, :, c * HEAD_DIM:(c + 1) * HEAD_DIM]
        vc = vv[:, :, c * HEAD_DIM:(c + 1) * HEAD_DIM]
        qg = jnp.stack([qs[:, (c * GQA_GROUP + g) * HEAD_DIM:(c * GQA_GROUP + g + 1) * HEAD_DIM]
                        for g in range(GQA_GROUP)], axis=1).astype(BF16)
        s = jnp.einsum('bgd,bsd->bgs', qg, kc, preferred_element_type=F32)
        ps = []
        for g in range(GQA_GROUP):
            sg = s[:, g, :]
            sink = sinks_ref[c * GQA_GROUP + g]
            m = jnp.maximum(jnp.max(sg, axis=-1, keepdims=True), sink)
            p = jnp.exp(sg - m)
            den = jnp.sum(p, axis=-1, keepdims=True) + jnp.exp(sink - m)
            ps.append(p / den)
        pm = jnp.stack(ps, axis=1).astype(BF16)
        o = jnp.einsum('bgs,bsd->bgd', pm, vc, preferred_element_type=F32)
        for g in range(GQA_GROUP):
            heads.append(o[:, g, :])
    o_attn = jnp.concatenate(heads, axis=1)

    u = _gelu_tanh(_dot(h, w_in[:, OFF_GU:OFF_GV]) + b_in[:, OFF_GU:OFF_GV])
    vn = _layer_norm(_gelu_tanh(_dot(h, w_in[:, OFF_GV:OFF_GA]) + b_in[:, OFF_GV:OFF_GA]),
                     lnv_g[...], lnv_b[...])
    vn_ref[...] = vn
    o_gmlp = u * (ws0_ref[...] * vn + bs0_ref[...])

    ga = _dot(h, w_in[:, OFF_GA:OFF_GB]) + b_in[:, OFF_GA:OFF_GB]
    gb = _dot(h, w_in[:, OFF_GB:IN_W]) + b_in[:, OFF_GB:IN_W]
    x1, h2 = _mixer_tail(x, o_gmlp, o_attn, ga, gb, g1, sc2, sh2, w_pa, w_pb, w_o, b_o, ln1_g, ln1_b)
    x1_ref[...] = x1
    h2_ref[...] = h2
    meta, gates, new_base = _route(h2, wr_hi, wr_lo, br, base_ref[...])
    meta_ref[...] = meta
    gates_ref[...] = gates
    cnt_ref[...] = new_base


def _front_sample(x2d, mod_s, cos_s, sin_s, ck, cv, base, sinks, wts):
    db = x2d.shape[0]
    args = (x2d, mod_s, cos_s, sin_s, ck, cv, base) + tuple(wts)
    in_specs = [pl.BlockSpec(memory_space=pltpu.SMEM)] + [_const_spec(a.shape) for a in args]
    out_shape = [
        jax.ShapeDtypeStruct((db, D_MODEL), F32),
        jax.ShapeDtypeStruct((db, D_MODEL), F32),
        jax.ShapeDtypeStruct((db, META_W), jnp.int32),
        jax.ShapeDtypeStruct((db, META_W), F32),
        jax.ShapeDtypeStruct((db, KV_W), F32),
        jax.ShapeDtypeStruct((db, KV_W), F32),
        jax.ShapeDtypeStruct((db, GMLP_W), F32),
        jax.ShapeDtypeStruct((1, N_EXPERTS), F32),
    ]
    return pl.pallas_call(
        _front_sample_kernel,
        grid=(1,),
        in_specs=in_specs,
        out_specs=[_const_spec(s.shape) for s in out_shape],
        out_shape=out_shape,
        compiler_params=pltpu.CompilerParams(dimension_semantics=("arbitrary",),
                                             vmem_limit_bytes=VMEM_LIMIT),
        name="front_sample",
    )(sinks, *args)


def _row_copy(src, t, dst, d, sem):
    return pltpu.make_async_copy(src.at[pl.ds(t, 1)], dst.at[pl.ds(d, 1)], sem)


def _dispatch_kernel(dest_ref, hp_ref, hs_ref, xs_ref, sem, *, n_prompt_tiles):
    i = pl.program_id(0)

    def scatter(src):
        def start(g, carry):
            for u in range(DMA_UNROLL):
                t = g * DMA_UNROLL + u
                for k in range(TOP_K):
                    _row_copy(src, t, xs_ref, dest_ref[0, t * TOP_K + k], sem).start()
            return carry
        lax.fori_loop(0, TD // DMA_UNROLL, start, 0)

        def wait(g, carry):
            for _ in range(DMA_UNROLL * TOP_K):
                _row_copy(src, 0, xs_ref, 0, sem).wait()
            return carry
        lax.fori_loop(0, TD // DMA_UNROLL, wait, 0)

    @pl.when(i < n_prompt_tiles)
    def _():
        scatter(hp_ref)

    @pl.when(i >= n_prompt_tiles)
    def _():
        scatter(hs_ref)


def _dispatch(dest_tiles, h2_p, h2_s):
    npt = h2_p.shape[0] // TD
    nst = h2_s.shape[0] // TD
    rows = (h2_p.shape[0] + h2_s.shape[0]) * TOP_K
    return pl.pallas_call(
        functools.partial(_dispatch_kernel, n_prompt_tiles=npt),
        grid=(npt + nst,),
        in_specs=[
            pl.BlockSpec((None, 1, TD * TOP_K), lambda i: (i, 0, 0), memory_space=pltpu.SMEM),
            pl.BlockSpec((TD, D_MODEL), lambda i: (jnp.minimum(i, npt - 1), 0)),
            pl.BlockSpec((TD, D_MODEL), lambda i: (jnp.maximum(i - npt, 0), 0)),
        ],
        out_specs=pl.BlockSpec(memory_space=pl.ANY),
        out_shape=jax.ShapeDtypeStruct((rows, D_MODEL), F32),
        scratch_shapes=[pltpu.SemaphoreType.DMA(())],
        compiler_params=pltpu.CompilerParams(dimension_semantics=("arbitrary",)),
        name="dispatch",
    )(dest_tiles, h2_p, h2_s)


def _moe_kernel(blk_ref, exp_ref, lo_ref, hi_ref, flag_ref, x_ref, w1_ref, b1_ref, w2_ref, b2_ref, o_ref,
                w1b, w2b):
    del blk_ref, exp_ref
    s = pl.program_id(0)
    flag = flag_ref[s]

    @pl.when((flag & 1) != 0)
    def _():
        @pl.when((flag & 4) != 0)
        def _():
            w1b[...] = w1_ref[...].astype(BF16)
            w2b[...] = w2_ref[...].astype(BF16)

        @pl.when((flag & 2) != 0)
        def _():
            o_ref[...] = jnp.zeros_like(o_ref)

        x = x_ref[...].astype(BF16)
        z = _dot(x, w1b[...]) + b1_ref[...]
        zg = jnp.minimum(z[:, :D_FF], SWIGLU_LIMIT)
        zl = jnp.clip(z[:, D_FF:], -SWIGLU_LIMIT, SWIGLU_LIMIT)
        act = zg * jax.nn.sigmoid(SWIGLU_ALPHA * zg) * (zl + 1.0)
        y = _dot(act.astype(BF16), w2b[...]) + b2_ref[...]
        row = lax.broadcasted_iota(jnp.int32, (BM, 1), 0)
        mine = (row >= lo_ref[s]) & (row < hi_ref[s])
        o_ref[...] = jnp.where(mine, y, o_ref[...])


def _moe(tables, xs, w1, b1, w2, b2):
    blk, exp, lo, hi, flag = tables
    nsteps = blk.shape[0]
    grid_spec = pltpu.PrefetchScalarGridSpec(
        num_scalar_prefetch=5,
        grid=(nsteps,),
        in_specs=[
            pl.BlockSpec((BM, D_MODEL), lambda s, blk, exp, lo, hi, fl: (blk[s], 0)),
            pl.BlockSpec((None, D_MODEL, 2 * D_FF), lambda s, blk, exp, lo, hi, fl: (exp[s], 0, 0)),
            pl.BlockSpec((None, 1, 2 * D_FF), lambda s, blk, exp, lo, hi, fl: (exp[s], 0, 0)),
            pl.BlockSpec((None, D_FF, D_MODEL), lambda s, blk, exp, lo, hi, fl: (exp[s], 0, 0)),
            pl.BlockSpec((None, 1, D_MODEL), lambda s, blk, exp, lo, hi, fl: (exp[s], 0, 0)),
        ],
        out_specs=pl.BlockSpec((BM, D_MODEL), lambda s, blk, exp, lo, hi, fl: (blk[s], 0)),
        scratch_shapes=[pltpu.VMEM((D_MODEL, 2 * D_FF), BF16), pltpu.VMEM((D_FF, D_MODEL), BF16)],
    )
    return pl.pallas_call(
        _moe_kernel,
        grid_spec=grid_spec,
        out_shape=jax.ShapeDtypeStruct(xs.shape, F32),
        compiler_params=pltpu.CompilerParams(dimension_semantics=("arbitrary",),
                                             vmem_limit_bytes=VMEM_LIMIT),
        name="moe",
    )(blk, exp, lo, hi, flag, xs, w1, b1, w2, b2)


def _moe_tables(counts, nsteps):
    counts = counts.astype(jnp.int32)
    ends = jnp.cumsum(counts)
    starts = ends - counts
    first_blk = starts // BM
    last_blk = (ends - 1) // BM
    n_e = jnp.where(counts > 0, last_blk - first_blk + 1, 0)
    step_end = jnp.cumsum(n_e)
    step_start = step_end - n_e
    total = step_end[-1]
    s = jnp.arange(nsteps, dtype=jnp.int32)
    valid = s < total
    sc = jnp.minimum(s, total - 1)
    e = jnp.minimum(jnp.sum((step_end[None, :] <= sc[:, None]).astype(jnp.int32), axis=1), N_EXPERTS - 1)
    onehot = e[:, None] == jnp.arange(N_EXPERTS, dtype=jnp.int32)[None, :]
    pick = lambda tbl: jnp.sum(jnp.where(onehot, tbl[None, :], 0), axis=1)
    blk = pick(first_blk) + (sc - pick(step_start))
    lo = jnp.maximum(pick(starts), blk * BM) - blk * BM
    hi = jnp.minimum(pick(ends), (blk + 1) * BM) - blk * BM
    lo = jnp.where(valid, lo, 0)
    hi = jnp.where(valid, hi, 0)
    prev_blk = jnp.concatenate([jnp.full((1,), -1, jnp.int32), blk[:-1]])
    prev_e = jnp.concatenate([jnp.full((1,), -1, jnp.int32), e[:-1]])
    flag = (valid.astype(jnp.int32) + 2 * (valid & (blk != prev_blk)).astype(jnp.int32)
            + 4 * (valid & (e != prev_e)).astype(jnp.int32))
    return starts, (blk.astype(jnp.int32), e, lo.astype(jnp.int32), hi.astype(jnp.int32), flag)


def _combine_kernel(dest_ref, x1_ref, gates_ref, g2_ref, ln2_g, ln2_b, ys_ref, o_ref, rows, sem):
    def start(g, carry):
        for u in range(DMA_UNROLL):
            t = g * DMA_UNROLL + u
            for k in range(TOP_K):
                pltpu.make_async_copy(ys_ref.at[pl.ds(dest_ref[0, t * TOP_K + k], 1)],
                                      rows.at[k, pl.ds(t, 1)], sem).start()
        return carry
    lax.fori_loop(0, TD // DMA_UNROLL, start, 0)

    def wait(g, carry):
        for _ in range(DMA_UNROLL * TOP_K):
            pltpu.make_async_copy(ys_ref.at[pl.ds(0, 1)], rows.at[0, pl.ds(0, 1)], sem).wait()
        return carry
    lax.fori_loop(0, TD // DMA_UNROLL, wait, 0)

    gates = gates_ref[...]
    ff = gates[:, 0:1] * rows[0]
    for k in range(1, TOP_K):
        ff = ff + gates[:, k:k + 1] * rows[k]
    o_ref[...] = _layer_norm(DN_ALPHA * x1_ref[...] + g2_ref[...] * ff, ln2_g[...], ln2_b[...])


def _combine(dest_tiles, x1, gates, g2, g2_spec, ln2_g, ln2_b, ys):
    nt = x1.shape[0] // TD
    return pl.pallas_call(
        _combine_kernel,
        grid=(nt,),
        in_specs=[
            pl.BlockSpec((None, 1, TD * TOP_K), lambda i: (i, 0, 0), memory_space=pltpu.SMEM),
            pl.BlockSpec((TD, D_MODEL), lambda i: (i, 0)),
            pl.BlockSpec((TD, META_W), lambda i: (i, 0)),
            g2_spec,
            pl.BlockSpec((1, D_MODEL), lambda i: (0, 0)),
            pl.BlockSpec((1, D_MODEL), lambda i: (0, 0)),
            pl.BlockSpec(memory_space=pl.ANY),
        ],
        out_specs=pl.BlockSpec((TD, D_MODEL), lambda i: (i, 0)),
        out_shape=jax.ShapeDtypeStruct(x1.shape, F32),
        scratch_shapes=[pltpu.VMEM((TOP_K, TD, D_MODEL), F32), pltpu.SemaphoreType.DMA(())],
        compiler_params=pltpu.CompilerParams(dimension_semantics=("arbitrary",)),
        name="combine",
    )(dest_tiles, x1, gates, g2, ln2_g, ln2_b, ys)


def _rope_tables(pos):
    half = HEAD_DIM // 2
    inv = np.float32(ROPE_THETA) ** (-np.arange(half, dtype=np.float32) / np.float32(half))
    ang = (np.asarray(pos, np.float32)[:, None] * inv[None, :]).astype(np.float64)
    cos, sin = np.cos(ang), np.sin(ang)
    cos_t = np.tile(np.concatenate([cos, cos], -1), (1, 128 // HEAD_DIM)).astype(np.float32)
    sin_t = np.tile(np.concatenate([-sin, sin], -1), (1, 128 // HEAD_DIM)).astype(np.float32)
    return jnp.asarray(cos_t), jnp.asarray(sin_t)


def kernel(x_prompt, x_sample, cache_k_win, cache_v_win, c_prompt, c_sample, w_ada, b_ada, w_in, b_in, sinks,
           ln_v_g, ln_v_b, w_s, b_s, w_pa, w_pb, w_o, b_o, ln1_g, ln1_b, ln2_g, ln2_b,
           w_router, b_router, w1, b1, w2, b2):
    assert w_ada.shape[0] == DEPTH
    batch, seq, _ = x_prompt.shape
    db, dl, _ = x_sample.shape
    assert dl == 1 and seq % TM == 0 and db % TD == 0 and cache_k_win.shape[2] == WINDOW
    t_p = batch * seq
    t_all = t_p + db

    n_seq = batch + db
    n_seq_pad = -(-n_seq // 8) * 8
    c_all = jnp.concatenate([c_prompt, c_sample, jnp.zeros((n_seq_pad - n_seq, D_MODEL), F32)], 0)
    mod = _ada(c_all, w_ada[0], b_ada)
    mod_p = mod[:batch].reshape(batch, 1, 6 * D_MODEL)
    mod_s = mod[batch:n_seq]

    wr = w_router[0]
    wr_hi = wr.astype(BF16)
    wr_lo = (wr - wr_hi.astype(F32)).astype(BF16)
    shared = (w_pa[0].astype(BF16), w_pb[0].astype(BF16), w_o[0].astype(BF16), b_o, ln1_g, ln1_b,
              wr_hi, wr_lo, b_router)
    w_in_b = w_in[0].astype(BF16)
    sinks1 = sinks[0]

    cos_p, sin_p = _rope_tables(np.arange(seq))
    bs_t = jnp.repeat(jnp.transpose(b_s[0]), GROUP_CH, axis=1)
    wts_p = (w_in_b, b_in, ln_v_g, ln_v_b, w_s[0], bs_t) + shared
    x1_p, h2_p, meta_p, gates_p, kwin, vwin, cnt_p = _front_prompt(
        x_prompt.reshape(t_p, D_MODEL), mod_p, cos_p, sin_p, sinks1, wts_p, batch, seq)

    cos_s, sin_s = _rope_tables(np.full((1,), PAST_LEN))
    ws0 = jnp.repeat(w_s[0, :, 0, 0], GROUP_CH)[None, :]
    bs0 = jnp.repeat(b_s[0, :, 0], GROUP_CH)[None, :]
    wts_s = (w_in_b, b_in, ln_v_g, ln_v_b, ws0, bs0) + shared
    x1_s, h2_s, meta_s, gates_s, k_new, v_new, vn_s, cnt_all = _front_sample(
        x_sample.reshape(db, D_MODEL), mod_s, cos_s, sin_s,
        cache_k_win[0].reshape(db, WINDOW, KV_W), cache_v_win[0].reshape(db, WINDOW, KV_W),
        cnt_p, sinks1, wts_s)

    nsteps = t_all * TOP_K // BM + N_EXPERTS - 1
    starts, tables = _moe_tables(cnt_all[0], nsteps)
    meta = jnp.concatenate([meta_p, meta_s], 0)
    onehot = meta[:, :TOP_K, None] == jnp.arange(N_EXPERTS, dtype=jnp.int32)[None, None, :]
    dest = jnp.sum(jnp.where(onehot, starts[None, None, :], 0), axis=-1) + meta[:, TOP_K:]
    dest_tiles = dest.reshape(t_all // TD, 1, TD * TOP_K)
    xs = _dispatch(dest_tiles, h2_p, h2_s)
    ys = _moe(tables, xs, w1[0], b1[0][:, None, :], w2[0], b2[0][:, None, :])

    g2_p = mod_p[:, :, 5 * D_MODEL:]
    tiles_per_seq = seq // TD
    y_p = _combine(dest_tiles[:t_p // TD], x1_p, gates_p, g2_p,
                   pl.BlockSpec((None, 1, D_MODEL), lambda i: (i // tiles_per_seq, 0, 0)),
                   ln2_g, ln2_b, ys)
    y_s = _combine(dest_tiles[t_p // TD:], x1_s, gates_s, mod_s[:, 5 * D_MODEL:],
                   pl.BlockSpec((TD, D_MODEL), lambda i: (i, 0)), ln2_g, ln2_b, ys)

    return (y_p.reshape(batch, seq, D_MODEL),
            y_s.reshape(db, 1, D_MODEL),
            kwin.reshape(1, batch, WINDOW, N_KV_HEADS, HEAD_DIM),
            vwin.reshape(1, batch, WINDOW, N_KV_HEADS, HEAD_DIM),
            k_new.reshape(1, db, 1, N_KV_HEADS, HEAD_DIM),
            v_new.reshape(1, db, 1, N_KV_HEADS, HEAD_DIM),
            vn_s.reshape(1, db, 1, GMLP_W))
```
